```python
import math
import jax, jax.numpy as jnp
from jax import lax
import numpy as np

D_MODEL = 1024
BATCH = 16
SEQ = 256
DEPTH = 2
DEC_BATCH = 4
DEC_SEQ = 4096
PAST_LEN = 256

GRID_W = 64
HEAD_DIM = 64
NA_HEADS = 8
NA_WIDTH = NA_HEADS * HEAD_DIM
WIN_H = 8
WIN_W = 16
SSM_WIDTH = D_MODEL - NA_WIDTH
SSM_GROUP = 16
SSM_GROUPS = SSM_WIDTH // SSM_GROUP
SSM_STATE = 64
GQA_HEADS = 16
GQA_KV_HEADS = 4
ROPE_THETA = 10000.0
D_FF = 2816
Q_BLOCK = 128
EPS = 1e-6
N_EVEN = (DEPTH + 1) // 2
N_ODD = DEPTH // 2
EVEN_IN = 3 * NA_WIDTH + SSM_WIDTH
ODD_IN = (GQA_HEADS + 2 * GQA_KV_HEADS) * HEAD_DIM

kernel_name = 'hybrid_natten_s5_gqa_diffusion_step'


def rmsnorm(x, g):
    xf = x.astype(jnp.float32)
    xf = xf * lax.rsqrt(jnp.mean(xf * xf, axis=-1, keepdims=True) + EPS)
    return (xf * g.astype(jnp.float32)).astype(x.dtype)


def adaln(cvec, w, b):
    m = jnp.dot(jax.nn.silu(cvec), w) + b
    return jnp.split(m[..., None, :], 6, axis=-1)


def modulate(h, shift, scale):
    return h * (1 + scale) + shift


def rope_2d(x):
    L = x.shape[1]
    t = jnp.arange(L)
    half = HEAD_DIM // 2
    nf = half // 2
    freqs = ROPE_THETA ** (-jnp.arange(nf, dtype=jnp.float32) / nf)

    def rotate(xh, pos):
        ang = pos.astype(jnp.float32)[:, None] * freqs[None, :]
        cos = jnp.cos(ang)[None, :, None, :]
        sin = jnp.sin(ang)[None, :, None, :]
        x1, x2 = xh[..., :nf], xh[..., nf:]
        return jnp.concatenate([x1 * cos - x2 * sin, x1 * sin + x2 * cos], axis=-1)

    xf = x.astype(jnp.float32)
    out = jnp.concatenate([rotate(xf[..., :half], t // GRID_W), rotate(xf[..., half:], t % GRID_W)], axis=-1)
    return out.astype(x.dtype)


def blocked_attention(q, k, v):
    B, S, H, Dh = q.shape
    Hkv = k.shape[2]
    G = H // Hkv
    nb = S // Q_BLOCK
    qb = q.reshape(B, nb, Q_BLOCK, Hkv, G, Dh).transpose(1, 0, 2, 3, 4, 5)
    scale = Dh ** -0.5

    def block(qi):
        s = jnp.einsum('bqkgd,btkd->bkgqt', qi, k, preferred_element_type=jnp.float32) * scale
        p = jax.nn.softmax(s, axis=-1).astype(v.dtype)
        return jnp.einsum('bkgqt,btkd->bqkgd', p, v)

    o = lax.map(block, qb)
    return o.transpose(1, 0, 2, 3, 4, 5).reshape(B, S, H, Dh)


def na_latent(q, k, v, k_ctx, v_ctx, rpb):
    B, L, H, Dh = q.shape
    rows = L // GRID_W
    kh = min(WIN_H, rows)
    kw = WIN_W
    qg = q.reshape(B, rows, GRID_W, H, Dh)
    kg = k.reshape(B, rows, GRID_W, H, Dh)
    vg = v.reshape(B, rows, GRID_W, H, Dh)
    cols = jnp.arange(GRID_W)
    col_start = jnp.clip(cols - kw // 2, 0, GRID_W - kw)
    col_idx = col_start[:, None] + jnp.arange(kw)[None, :]
    dc = col_idx - cols[:, None] + (WIN_W - 1)
    rpb_f = rpb.astype(jnp.float32)
    scale = Dh ** -0.5

    def row_fn(r):
        rs = jnp.clip(r - kh // 2, 0, rows - kh)
        q_r = lax.dynamic_index_in_dim(qg, r, axis=1, keepdims=False)
        k_band = lax.dynamic_slice_in_dim(kg, rs, kh, axis=1)
        v_band = lax.dynamic_slice_in_dim(vg, rs, kh, axis=1)
        k_win = k_band[:, :, col_idx]
        v_win = v_band[:, :, col_idx]
        s_loc = jnp.einsum('bwhd,biwjhd->bhwij', q_r, k_win, preferred_element_type=jnp.float32) * scale
        dr = rs + jnp.arange(kh) - r + (WIN_H - 1)
        bias = rpb_f[:, dr[None, :, None], dc[:, None, :]]
        s_loc = s_loc + bias[None]
        s_ctx = jnp.einsum('bwhd,blhd->bhwl', q_r, k_ctx, preferred_element_type=jnp.float32) * scale
        s = jnp.concatenate([s_loc.reshape(B, H, GRID_W, kh * kw), s_ctx], axis=-1)
        p = jax.nn.softmax(s, axis=-1).astype(v.dtype)
        p_loc = p[..., :kh * kw].reshape(B, H, GRID_W, kh, kw)
        p_ctx = p[..., kh * kw:]
        return (jnp.einsum('bhwij,biwjhd->bwhd', p_loc, v_win)
                + jnp.einsum('bhwl,blhd->bwhd', p_ctx, v_ctx))

    out = lax.map(row_fn, jnp.arange(rows))
    return out.transpose(1, 0, 2, 3, 4).reshape(B, L, H, Dh)


def zoh(a_re, a_im, log_dt, b_re, b_im):
    a_re = a_re.astype(jnp.float32)
    a_im = a_im.astype(jnp.float32)
    b_re = b_re.astype(jnp.float32)
    b_im = b_im.astype(jnp.float32)
    dt = jnp.exp(log_dt.astype(jnp.float32))[:, None]
    mag = jnp.exp(a_re * dt)
    abr = mag * jnp.cos(a_im * dt)
    abi = mag * jnp.sin(a_im * dt)
    den = a_re * a_re + a_im * a_im
    nr = abr - 1.0
    ni = abi
    kr = (nr * a_re + ni * a_im) / den
    ki = (ni * a_re - nr * a_im) / den
    bbr = kr[..., None] * b_re - ki[..., None] * b_im
    bbi = kr[..., None] * b_im + ki[..., None] * b_re
    return abr, abi, bbr, bbi


def _complex_affine_combine(e1, e2):
    a1r, a1i, b1r, b1i = e1
    a2r, a2i, b2r, b2i = e2
    return (a2r * a1r - a2i * a1i,
            a2r * a1i + a2i * a1r,
            a2r * b1r - a2i * b1i + b2r,
            a2r * b1i + a2i * b1r + b2i)


def diag_scan(abr, abi, bur, bui, h0, reverse):
    L = bur.shape[1]
    ar = jnp.broadcast_to(abr, (1, L) + abr.shape)
    ai = jnp.broadcast_to(abi, (1, L) + abi.shape)
    Ar, Ai, Hr, Hi = lax.associative_scan(_complex_affine_combine, (ar, ai, bur, bui), reverse=reverse, axis=1)
    if h0 is not None:
        h0r = h0[0][:, None]
        h0i = h0[1][:, None]
        Hr = Hr + Ar * h0r - Ai * h0i
        Hi = Hi + Ar * h0i + Ai * h0r
    return Hr, Hi


def s5_mixer(u, ssm, init):
    a_re, a_im, log_dt, b_re, b_im, c_re, c_im, d, w_glu, b_glu = ssm
    B, L, _ = u.shape
    uf = u.astype(jnp.float32).reshape(B, L, SSM_GROUPS, SSM_GROUP)
    y = uf * d.astype(jnp.float32)
    finals = []
    for dr in range(2):
        rev = dr == 1
        abr, abi, bbr, bbi = zoh(a_re[dr], a_im[dr], log_dt[dr], b_re[dr], b_im[dr])
        bur = jnp.einsum('blgc,gpc->blgp', uf, bbr)
        bui = jnp.einsum('blgc,gpc->blgp', uf, bbi)
        h0 = None if init is None else (init[0][:, dr].astype(jnp.float32), init[1][:, dr].astype(jnp.float32))
        hr, hi = diag_scan(abr, abi, bur, bui, h0, rev)
        y = (y + jnp.einsum('gcp,blgp->blgc', c_re[dr].astype(jnp.float32), hr)
             - jnp.einsum('gcp,blgp->blgc', c_im[dr].astype(jnp.float32), hi))
        if init is None:
            end = 0 if rev else L - 1
            finals.append((hr[:, end], hi[:, end]))
    y = jax.nn.gelu(y.reshape(B, L, SSM_WIDTH))
    y = y * jax.nn.sigmoid(y @ w_glu.astype(jnp.float32) + b_glu.astype(jnp.float32))
    y = y.astype(u.dtype)
    if init is None:
        s_re = jnp.stack([f[0] for f in finals], axis=1).astype(u.dtype)
        s_im = jnp.stack([f[1] for f in finals], axis=1).astype(u.dtype)
        return y, (s_re, s_im)
    return y, None


def even_mixer(h, ev, ctx):
    w_in, w_out, q_g, k_g, rpb = ev[:5]
    ssm = ev[5:]
    B, L, _ = h.shape
    hp = h @ w_in
    q, k, v, u = jnp.split(hp, [NA_WIDTH, 2 * NA_WIDTH, 3 * NA_WIDTH], axis=-1)
    q = rmsnorm(q.reshape(B, L, NA_HEADS, HEAD_DIM), q_g)
    k = rmsnorm(k.reshape(B, L, NA_HEADS, HEAD_DIM), k_g)
    v = v.reshape(B, L, NA_HEADS, HEAD_DIM)
    if ctx is None:
        na = blocked_attention(q, k, v)
        y_ssm, (s_re, s_im) = s5_mixer(u, ssm, None)
        out = jnp.concatenate([na.reshape(B, L, NA_WIDTH), y_ssm], axis=-1) @ w_out
        return out, (k, v, s_re, s_im)
    k_ctx, v_ctx, s_re, s_im = ctx
    na = na_latent(q, k, v, k_ctx, v_ctx, rpb)
    y_ssm, _ = s5_mixer(u, ssm, (s_re, s_im))
    return jnp.concatenate([na.reshape(B, L, NA_WIDTH), y_ssm], axis=-1) @ w_out, None


def odd_mixer(h, od, ctx):
    w_in, w_out, q_g, k_g = od
    B, L, _ = h.shape
    hp = h @ w_in
    q, k, v = jnp.split(hp, [GQA_HEADS * HEAD_DIM, (GQA_HEADS + GQA_KV_HEADS) * HEAD_DIM], axis=-1)
    q = rmsnorm(q.reshape(B, L, GQA_HEADS, HEAD_DIM), q_g)
    k = rmsnorm(k.reshape(B, L, GQA_KV_HEADS, HEAD_DIM), k_g)
    v = v.reshape(B, L, GQA_KV_HEADS, HEAD_DIM)
    if ctx is None:
        o = blocked_attention(q, k, v)
        return o.reshape(B, L, GQA_HEADS * HEAD_DIM) @ w_out, (k, v)
    k_ctx, v_ctx = ctx
    q = rope_2d(q)
    k = rope_2d(k)
    o = blocked_attention(q, jnp.concatenate([k, k_ctx], axis=1), jnp.concatenate([v, v_ctx], axis=1))
    return o.reshape(B, L, GQA_HEADS * HEAD_DIM) @ w_out, None


def dwconv3(x, w, b):
    xp = jnp.pad(x, ((0, 0), (1, 1), (0, 0)))
    return xp[:, :-2] * w[0] + xp[:, 1:-1] * w[1] + xp[:, 2:] * w[2] + b


def conv_ffn(h, w_up, conv_w, conv_b, w_down):
    gate, val = jnp.split(h @ w_up, 2, axis=-1)
    gate = dwconv3(gate, conv_w, conv_b)
    return (jax.nn.silu(gate) * val) @ w_down


def setup_inputs(seed: int = 0) -> dict:
    key = jax.random.key(seed)
    ks = jax.random.split(key, 40)

    def nrm(k, shape, s):
        return jax.random.normal(k, shape, jnp.float32) * s

    G, P, C = SSM_GROUPS, SSM_STATE, SSM_GROUP
    n_idx = jnp.arange(P, dtype=jnp.float32)
    return {
        'x_prompt': nrm(ks[0], (BATCH, SEQ, D_MODEL), 1.0),
        'x_sample': nrm(ks[1], (DEC_BATCH, DEC_SEQ, D_MODEL), 1.0),
        'cache_na_k': nrm(ks[2], (DEC_BATCH, N_EVEN, PAST_LEN, NA_HEADS, HEAD_DIM), 1.0),
        'cache_na_v': nrm(ks[3], (DEC_BATCH, N_EVEN, PAST_LEN, NA_HEADS, HEAD_DIM), 1.0),
        'state_ssm_re': nrm(ks[4], (DEC_BATCH, N_EVEN, 2, G, P), 0.3),
        'state_ssm_im': nrm(ks[5], (DEC_BATCH, N_EVEN, 2, G, P), 0.3),
        'cache_gqa_k': nrm(ks[6], (DEC_BATCH, N_ODD, PAST_LEN, GQA_KV_HEADS, HEAD_DIM), 1.0),
        'cache_gqa_v': nrm(ks[7], (DEC_BATCH, N_ODD, PAST_LEN, GQA_KV_HEADS, HEAD_DIM), 1.0),
        'c': nrm(ks[8], (DEC_BATCH, D_MODEL), 1.0),
        'c_ctx': nrm(ks[9], (D_MODEL,), 1.0),
        'norm1_g': 1.0 + nrm(ks[10], (DEPTH, D_MODEL), 0.02),
        'norm2_g': 1.0 + nrm(ks[11], (DEPTH, D_MODEL), 0.02),
        'ada_w': nrm(ks[12], (DEPTH, D_MODEL, 6 * D_MODEL), D_MODEL ** -0.5),
        'ada_b': nrm(ks[13], (DEPTH, 6 * D_MODEL), 0.01),
        'ffn_w_up': nrm(ks[14], (DEPTH, D_MODEL, 2 * D_FF), D_MODEL ** -0.5),
        'ffn_conv_w': nrm(ks[15], (DEPTH, 3, D_FF), 3 ** -0.5),
        'ffn_conv_b': nrm(ks[16], (DEPTH, D_FF), 0.01),
        'ffn_w_down': nrm(ks[17], (DEPTH, D_FF, D_MODEL), D_FF ** -0.5),
        'ev_w_in': nrm(ks[18], (N_EVEN, D_MODEL, EVEN_IN), D_MODEL ** -0.5),
        'ev_w_out': nrm(ks[19], (N_EVEN, NA_WIDTH + SSM_WIDTH, D_MODEL), (NA_WIDTH + SSM_WIDTH) ** -0.5),
        'na_q_g': 1.0 + nrm(ks[20], (N_EVEN, HEAD_DIM), 0.02),
        'na_k_g': 1.0 + nrm(ks[21], (N_EVEN, HEAD_DIM), 0.02),
        'na_rpb': nrm(ks[22], (N_EVEN, NA_HEADS, 2 * WIN_H - 1, 2 * WIN_W - 1), 0.02),
        'ssm_a_re': -0.5 + nrm(ks[23], (N_EVEN, 2, G, P), 0.01),
        'ssm_a_im': math.pi * n_idx + nrm(ks[24], (N_EVEN, 2, G, P), 0.01),
        'ssm_log_dt': jax.random.uniform(ks[25], (N_EVEN, 2, G), jnp.float32, math.log(1e-3), math.log(1e-1)),
        'ssm_b_re': nrm(ks[26], (N_EVEN, 2, G, P, C), (2 * C) ** -0.5),
        'ssm_b_im': nrm(ks[27], (N_EVEN, 2, G, P, C), (2 * C) ** -0.5),
        'ssm_c_re': nrm(ks[28], (N_EVEN, 2, G, C, P), (2 * P) ** -0.5),
        'ssm_c_im': nrm(ks[29], (N_EVEN, 2, G, C, P), (2 * P) ** -0.5),
        'ssm_d': nrm(ks[30], (N_EVEN, G, C), 1.0),
        'ssm_w_glu': nrm(ks[31], (N_EVEN, SSM_WIDTH, SSM_WIDTH), SSM_WIDTH ** -0.5),
        'ssm_b_glu': nrm(ks[32], (N_EVEN, SSM_WIDTH), 0.01),
        'od_w_in': nrm(ks[33], (N_ODD, D_MODEL, ODD_IN), D_MODEL ** -0.5),
        'od_w_out': nrm(ks[34], (N_ODD, GQA_HEADS * HEAD_DIM, D_MODEL), (GQA_HEADS * HEAD_DIM) ** -0.5),
        'gqa_q_g': 1.0 + nrm(ks[35], (N_ODD, HEAD_DIM), 0.02),
        'gqa_k_g': 1.0 + nrm(ks[36], (N_ODD, HEAD_DIM), 0.02),
    }


def reference(x_prompt, x_sample, cache_na_k, cache_na_v, state_ssm_re, state_ssm_im, cache_gqa_k, cache_gqa_v,
              c, c_ctx, norm1_g, norm2_g, ada_w, ada_b, ffn_w_up, ffn_conv_w, ffn_conv_b, ffn_w_down,
              ev_w_in, ev_w_out, na_q_g, na_k_g, na_rpb, ssm_a_re, ssm_a_im, ssm_log_dt, ssm_b_re, ssm_b_im,
              ssm_c_re, ssm_c_im, ssm_d, ssm_w_glu, ssm_b_glu, od_w_in, od_w_out, gqa_q_g, gqa_k_g):

    def even_params(i):
        return (ev_w_in[i], ev_w_out[i], na_q_g[i], na_k_g[i], na_rpb[i],
                ssm_a_re[i], ssm_a_im[i], ssm_log_dt[i], ssm_b_re[i], ssm_b_im[i],
                ssm_c_re[i], ssm_c_im[i], ssm_d[i], ssm_w_glu[i], ssm_b_glu[i])

    def odd_params(i):
        return (od_w_in[i], od_w_out[i], gqa_q_g[i], gqa_k_g[i])

    def layer(l, x, cond, ctx):
        sh1, sc1, g1, sh2, sc2, g2 = adaln(cond, ada_w[l], ada_b[l])
        h = modulate(rmsnorm(x, norm1_g[l]), sh1, sc1)
        if l % 2 == 0:
            out, cache = even_mixer(h, even_params(l // 2), ctx)
        else:
            out, cache = odd_mixer(h, odd_params(l // 2), ctx)
        x = x + g1 * out
        h = modulate(rmsnorm(x, norm2_g[l]), sh2, sc2)
        x = x + g2 * conv_ffn(h, ffn_w_up[l], ffn_conv_w[l], ffn_conv_b[l], ffn_w_down[l])
        return x, cache

    xp = x_prompt
    na_k, na_v, s_re, s_im, g_k, g_v = [], [], [], [], [], []
    for l in range(DEPTH):
        xp, cache = layer(l, xp, c_ctx, None)
        if l % 2 == 0:
            na_k.append(cache[0]); na_v.append(cache[1]); s_re.append(cache[2]); s_im.append(cache[3])
        else:
            g_k.append(cache[0]); g_v.append(cache[1])
    y_prompt = xp

    xs = x_sample
    for l in range(DEPTH):
        i = l // 2
        if l % 2 == 0:
            ctx = (cache_na_k[:, i], cache_na_v[:, i], state_ssm_re[:, i], state_ssm_im[:, i])
        else:
            ctx = (cache_gqa_k[:, i], cache_gqa_v[:, i])
        xs, _ = layer(l, xs, c, ctx)
    y_sample = xs

    new_na_k = jnp.stack(na_k, axis=1)
    new_na_v = jnp.stack(na_v, axis=1)
    new_ssm_re = jnp.stack(s_re, axis=1)
    new_ssm_im = jnp.stack(s_im, axis=1)
    new_gqa_k = jnp.stack(g_k, axis=1)
    new_gqa_v = jnp.stack(g_v, axis=1)
    return (y_prompt, y_sample, new_na_k, new_na_v, new_ssm_re, new_ssm_im, new_gqa_k, new_gqa_v)
```

```python
import functools
import math

import jax
import jax.numpy as jnp
from jax import lax
from jax.experimental import pallas as pl
from jax.experimental.pallas import tpu as pltpu

F32 = jnp.float32
BF16 = jnp.bfloat16

D_MODEL = 1024
DEPTH = 2
GRID_W = 64
HEAD_DIM = 64
NA_HEADS = 8
NA_WIDTH = NA_HEADS * HEAD_DIM
WIN_H = 8
WIN_W = 16
SSM_WIDTH = D_MODEL - NA_WIDTH
SSM_GROUP = 16
SSM_GROUPS = SSM_WIDTH // SSM_GROUP
SSM_STATE = 64
GQA_HEADS = 16
GQA_KV_HEADS = 4
ROPE_THETA = 10000.0
D_FF = 2816
EPS = 1e-6
EVEN_IN = 3 * NA_WIDTH + SSM_WIDTH
ODD_IN = (GQA_HEADS + 2 * GQA_KV_HEADS) * HEAD_DIM

LANES = 128
SUBLANES = 8
MXU_DIM = 256
HEADS_PER_GROUP = MXU_DIM // HEAD_DIM
CHUNK = 16
CHUNK_ROWS = 256
NA_QROWS = 4
NA_BAND = 12
MASK_VALUE = -1e30
VMEM_LIMIT = 56 * 1024 * 1024
FFN_COLS = 256
ADA_COLS = 512


def _cparams(n_axes):
    return pltpu.CompilerParams(dimension_semantics=("arbitrary",) * n_axes, vmem_limit_bytes=VMEM_LIMIT)


def _resident(shape):
    zeros = (0,) * len(shape)
    return pl.BlockSpec(shape, lambda *_: zeros, pipeline_mode=pl.Buffered(1))


def _ada_kernel(c_ref, w_ref, b_ref, o_ref):
    s = jax.nn.silu(c_ref[...])
    o_ref[0] = jnp.dot(s, w_ref[0], preferred_element_type=F32, precision=lax.Precision.HIGHEST) + b_ref[0]


def _ada(cond, ada_w, ada_b):
    n = ada_w.shape[-1]
    return pl.pallas_call(
        _ada_kernel,
        grid=(DEPTH, n // ADA_COLS),
        in_specs=[
            pl.BlockSpec((SUBLANES, D_MODEL), lambda l, j: (0, 0)),
            pl.BlockSpec((1, D_MODEL, ADA_COLS), lambda l, j: (l, 0, j)),
            pl.BlockSpec((1, 1, ADA_COLS), lambda l, j: (l, 0, j)),
        ],
        out_specs=pl.BlockSpec((1, SUBLANES, ADA_COLS), lambda l, j: (l, 0, j)),
        out_shape=jax.ShapeDtypeStruct((DEPTH, SUBLANES, n), F32),
        compiler_params=_cparams(2),
        name="ada",
    )(cond, ada_w, ada_b.reshape(DEPTH, 1, n))


def _norm_mod(x, g, shift, scale):
    r = lax.rsqrt(jnp.mean(x * x, axis=-1, keepdims=True) + EPS)
    return (x * r * g) * (1.0 + scale) + shift


def _head_norm(z, ones_bd, gain):
    z2 = z * z
    hi = z2.astype(BF16)
    lo = (z2 - hi.astype(F32)).astype(BF16)
    outs = []
    for c in range(z.shape[-1] // MXU_DIM):
        sl = slice(MXU_DIM * c, MXU_DIM * (c + 1))
        ss = (jnp.dot(hi[:, sl], ones_bd, preferred_element_type=F32)
              + jnp.dot(lo[:, sl], ones_bd, preferred_element_type=F32))
        outs.append(z[:, sl] * lax.rsqrt(ss * (1.0 / HEAD_DIM) + EPS))
    out = outs[0] if len(outs) == 1 else jnp.concatenate(outs, axis=-1)
    return out * gain


def _rope(x, cos, sin_signed):
    lane = lax.broadcasted_iota(jnp.int32, (1, LANES), 1)
    low = (lane & 31) < 16
    outs = []
    for c in range(x.shape[-1] // LANES):
        xc = x[:, LANES * c:LANES * (c + 1)]
        t = c % (MXU_DIM // LANES)
        partner = jnp.where(low, pltpu.roll(xc, LANES - 16, 1), pltpu.roll(xc, 16, 1))
        outs.append(xc * cos[:, LANES * t:LANES * (t + 1)] + partner * sin_signed[:, LANES * t:LANES * (t + 1)])
    return jnp.concatenate(outs, axis=-1)


def _row_spec(tm, width):
    return pl.BlockSpec((tm, width), lambda i: (i, 0))


def _mod_spec(tiles_per_batch):
    return pl.BlockSpec((1, 6, D_MODEL), lambda i: (i // tiles_per_batch, 0, 0))


def _even_in_kernel(x_ref, g_ref, mod_ref, w_ref, ones_ref, qg_ref, kg_ref, q_ref, k_ref, v_ref, u_ref):
    h = _norm_mod(x_ref[...], g_ref[...], mod_ref[0, 0:1, :], mod_ref[0, 1:2, :])
    y = jnp.dot(h.astype(BF16), w_ref[...], preferred_element_type=F32)
    ones_bd = ones_ref[...]
    q = _head_norm(y[:, :NA_WIDTH], ones_bd, qg_ref[...])
    k = _head_norm(y[:, NA_WIDTH:2 * NA_WIDTH], ones_bd, kg_ref[...])
    q_ref[...] = (q * (HEAD_DIM ** -0.5)).astype(q_ref.dtype)
    k_ref[...] = k.astype(k_ref.dtype)
    v_ref[...] = y[:, 2 * NA_WIDTH:3 * NA_WIDTH].astype(v_ref.dtype)
    u_ref[...] = y[:, 3 * NA_WIDTH:]


def _even_in(x, g, mod, w, ones_bd, qg, kg, *, tm, tiles_per_batch, kv_dtype):
    n = x.shape[0]
    return pl.pallas_call(
        _even_in_kernel,
        grid=(n // tm,),
        in_specs=[
            _row_spec(tm, D_MODEL),
            _resident((1, D_MODEL)),
            _mod_spec(tiles_per_batch),
            _resident(w.shape),
            _resident(ones_bd.shape),
            _resident(qg.shape),
            _resident(kg.shape),
        ],
        out_specs=[_row_spec(tm, NA_WIDTH)] * 3 + [_row_spec(tm, SSM_WIDTH)],
        out_shape=[
            jax.ShapeDtypeStruct((n, NA_WIDTH), BF16),
            jax.ShapeDtypeStruct((n, NA_WIDTH), kv_dtype),
            jax.ShapeDtypeStruct((n, NA_WIDTH), kv_dtype),
            jax.ShapeDtypeStruct((n, SSM_WIDTH), F32),
        ],
        compiler_params=_cparams(1),
        name="even_in",
    )(x, g, mod, w, ones_bd, qg, kg)


def _odd_in_kernel(x_ref, g_ref, mod_ref, w_ref, ones_ref, qg_ref, kg_ref, *rest, rope):
    if rope:
        cos_ref, sin_ref, q_ref, k_ref, v_ref = rest
    else:
        q_ref, k_ref, v_ref = rest
    qw = GQA_HEADS * HEAD_DIM
    kw = GQA_KV_HEADS * HEAD_DIM
    h = _norm_mod(x_ref[...], g_ref[...], mod_ref[0, 0:1, :], mod_ref[0, 1:2, :])
    y = jnp.dot(h.astype(BF16), w_ref[...], preferred_element_type=F32)
    ones_bd = ones_ref[...]
    q = _head_norm(y[:, :qw], ones_bd, qg_ref[...])
    k = _head_norm(y[:, qw:qw + kw], ones_bd, kg_ref[...])
    if rope:
        q = _rope(q, cos_ref[...], sin_ref[...])
        k = _rope(k, cos_ref[...], sin_ref[...])
    q_ref[...] = (q * (HEAD_DIM ** -0.5)).astype(q_ref.dtype)
    k_ref[...] = k.astype(k_ref.dtype)
    v_ref[...] = y[:, qw + kw:].astype(v_ref.dtype)


def _odd_in(x, g, mod, w, ones_bd, qg, kg, rope_tables, *, tm, tiles_per_batch, kv_dtype):
    n = x.shape[0]
    qw = GQA_HEADS * HEAD_DIM
    kw = GQA_KV_HEADS * HEAD_DIM
    in_specs = [
        _row_spec(tm, D_MODEL),
        _resident((1, D_MODEL)),
        _mod_spec(tiles_per_batch),
        _resident(w.shape),
        _resident(ones_bd.shape),
        _resident(qg.shape),
        _resident(kg.shape),
    ]
    args = [x, g, mod, w, ones_bd, qg, kg]
    if rope_tables is not None:
        in_specs += [pl.BlockSpec((tm, MXU_DIM), lambda i: (i % tiles_per_batch, 0))] * 2
        args += list(rope_tables)
    return pl.pallas_call(
        functools.partial(_odd_in_kernel, rope=rope_tables is not None),
        grid=(n // tm,),
        in_specs=in_specs,
        out_specs=[_row_spec(tm, qw), _row_spec(tm, kw), _row_spec(tm, kw)],
        out_shape=[
            jax.ShapeDtypeStruct((n, qw), BF16),
            jax.ShapeDtypeStruct((n, kw), kv_dtype),
            jax.ShapeDtypeStruct((n, kw), kv_dtype),
        ],
        compiler_params=_cparams(1),
        name="odd_in",
    )(*args)


def _even_out_kernel(x_ref, mod_ref, na_ref, ys_ref, wglu_ref, bglu_ref, wout_ref, o_ref):
    y = jax.nn.gelu(ys_ref[...])
    y = y * jax.nn.sigmoid(jnp.dot(y.astype(BF16), wglu_ref[...], preferred_element_type=F32) + bglu_ref[...])
    out = (jnp.dot(na_ref[...], wout_ref[:NA_WIDTH, :], preferred_element_type=F32)
           + jnp.dot(y.astype(BF16), wout_ref[NA_WIDTH:, :], preferred_element_type=F32))
    o_ref[...] = x_ref[...] + mod_ref[0, 2:3, :] * out


def _even_out(x, mod, na, ys, wglu, bglu, wout, *, tm, tiles_per_batch):
    n = x.shape[0]
    return pl.pallas_call(
        _even_out_kernel,
        grid=(n // tm,),
        in_specs=[
            _row_spec(tm, D_MODEL),
            _mod_spec(tiles_per_batch),
            _row_spec(tm, NA_WIDTH),
            _row_spec(tm, SSM_WIDTH),
            _resident(wglu.shape),
            _resident(bglu.shape),
            _resident(wout.shape),
        ],
        out_specs=_row_spec(tm, D_MODEL),
        out_shape=jax.ShapeDtypeStruct((n, D_MODEL), F32),
        compiler_params=_cparams(1),
        name="even_out",
    )(x, mod, na, ys, wglu, bglu, wout)


def _odd_out_kernel(x_ref, mod_ref, a_ref, wout_ref, o_ref):
    out = jnp.dot(a_ref[...], wout_ref[...], preferred_element_type=F32)
    o_ref[...] = x_ref[...] + mod_ref[0, 2:3, :] * out


def _odd_out(x, mod, attn, wout, *, tm, tiles_per_batch):
    n = x.shape[0]
    return pl.pallas_call(
        _odd_out_kernel,
        grid=(n // tm,),
        in_specs=[
            _row_spec(tm, D_MODEL),
            _mod_spec(tiles_per_batch),
            _row_spec(tm, attn.shape[1]),
            _resident(wout.shape),
        ],
        out_specs=_row_spec(tm, D_MODEL),
        out_shape=jax.ShapeDtypeStruct((n, D_MODEL), F32),
        compiler_params=_cparams(1),
        name="odd_out",
    )(x, mod, attn, wout)


def _ffn_kernel(xp_ref, x_ref, xn_ref, g_ref, mod_ref, wup_ref, cw_ref, cb_ref, wdn_ref, o_ref, *, tm,
                tiles_per_batch):
    i = pl.program_id(0)
    pos = i % tiles_per_batch
    g = g_ref[...]
    shift = mod_ref[0, 3:4, :]
    scale = mod_ref[0, 4:5, :]
    x = x_ref[...]
    h = _norm_mod(x, g, shift, scale)
    hp = _norm_mod(xp_ref[...], g, shift, scale)
    hn = _norm_mod(xn_ref[...], g, shift, scale)
    h_b = h.astype(BF16)
    hext_b = jnp.concatenate([hp, h, hn], axis=0).astype(BF16)
    row = lax.broadcasted_iota(jnp.int32, (tm, 1), 0)
    has_prev = pos > 0
    has_next = pos < tiles_per_batch - 1
    acc = jnp.zeros((tm, D_MODEL), F32)
    for c in range(D_FF // FFN_COLS):
        lo = FFN_COLS * c
        gate_ext = jnp.dot(hext_b, wup_ref[:, lo:lo + FFN_COLS], preferred_element_type=F32)
        val = jnp.dot(h_b, wup_ref[:, D_FF + lo:D_FF + lo + FFN_COLS], preferred_element_type=F32)
        gate = gate_ext[SUBLANES:SUBLANES + tm, :]
        g_prev = jnp.where(has_prev, gate_ext[SUBLANES - 1:SUBLANES, :], 0.0)
        g_next = jnp.where(has_next, gate_ext[SUBLANES + tm:SUBLANES + tm + 1, :], 0.0)
        g_m1 = jnp.where(row == 0, g_prev, pltpu.roll(gate, 1, 0))
        g_p1 = jnp.where(row == tm - 1, g_next, pltpu.roll(gate, tm - 1, 0))
        conv = (g_m1 * cw_ref[0:1, lo:lo + FFN_COLS] + gate * cw_ref[1:2, lo:lo + FFN_COLS]
                + g_p1 * cw_ref[2:3, lo:lo + FFN_COLS] + cb_ref[:, lo:lo + FFN_COLS])
        act = (jax.nn.silu(conv) * val).astype(BF16)
        acc = acc + jnp.dot(act, wdn_ref[lo:lo + FFN_COLS, :], preferred_element_type=F32)
    o_ref[...] = x + mod_ref[0, 5:6, :] * acc


def _ffn(x, g, mod, wup, cw, cb, wdn, *, tm, tiles_per_batch):
    n = x.shape[0]
    blocks_per_tile = tm // SUBLANES
    last_block = n // SUBLANES - 1
    return pl.pallas_call(
        functools.partial(_ffn_kernel, tm=tm, tiles_per_batch=tiles_per_batch),
        grid=(n // tm,),
        in_specs=[
            pl.BlockSpec((SUBLANES, D_MODEL), lambda i: (jnp.maximum(i * blocks_per_tile - 1, 0), 0)),
            _row_spec(tm, D_MODEL),
            pl.BlockSpec((SUBLANES, D_MODEL), lambda i: (jnp.minimum((i + 1) * blocks_per_tile, last_block), 0)),
            _resident((1, D_MODEL)),
            _mod_spec(tiles_per_batch),
            _resident(wup.shape),
            _resident(cw.shape),
            _resident(cb.shape),
            _resident(wdn.shape),
        ],
        out_specs=_row_spec(tm, D_MODEL),
        out_shape=jax.ShapeDtypeStruct((n, D_MODEL), F32),
        compiler_params=_cparams(1),
        name="conv_ffn",
    )(x, x, x, g, mod, wup, cw, cb, wdn)


def _head_masks():
    lane_head = lax.broadcasted_iota(jnp.int32, (1, MXU_DIM), 1) >> 6
    return [lane_head == s for s in range(HEADS_PER_GROUP)]


def _group_attention(q, k, v, bias=None):
    acc = None
    for s, m in enumerate(_head_masks()):
        qs = jnp.where(m, q, jnp.zeros_like(q))
        sc = lax.dot_general(qs, k, (((1,), (1,)), ((), ())), preferred_element_type=F32)
        if bias is not None:
            sc = sc + bias(s)
        p = jnp.exp(sc - jnp.max(sc, axis=-1, keepdims=True))
        inv = 1.0 / jnp.sum(p, axis=-1, keepdims=True)
        o = jnp.dot(p.astype(BF16), v, preferred_element_type=F32)
        o = jnp.where(m, o * inv, 0.0)
        acc = o if acc is None else acc + o
    return acc


def _attn_kernel(q_ref, k_ref, v_ref, o_ref):
    o = _group_attention(q_ref[0], k_ref[0].astype(BF16), v_ref[0].astype(BF16))
    o_ref[0] = o.astype(o_ref.dtype)


def _attention(q, k, v, *, tq, shared_kv):
    b, s, qw = q.shape
    t = k.shape[1]
    kv_index = (lambda bi, gi, ji: (bi, 0, 0)) if shared_kv else (lambda bi, gi, ji: (bi, 0, gi))
    return pl.pallas_call(
        _attn_kernel,
        grid=(b, qw // MXU_DIM, s // tq),
        in_specs=[
            pl.BlockSpec((1, tq, MXU_DIM), lambda bi, gi, ji: (bi, ji, gi)),
            pl.BlockSpec((1, t, MXU_DIM), kv_index),
            pl.BlockSpec((1, t, MXU_DIM), kv_index),
        ],
        out_specs=pl.BlockSpec((1, tq, MXU_DIM), lambda bi, gi, ji: (bi, ji, gi)),
        out_shape=jax.ShapeDtypeStruct((b, s, qw), BF16),
        compiler_params=_cparams(3),
        name="attention",
    )(q, k, v)


def _na_band_start(j, rows):
    return jnp.clip(j * NA_QROWS - WIN_H // 2, 0, rows - NA_BAND)


def _na_kernel(q_ref, k_ref, v_ref, kc_ref, vc_ref, bias_ref, o_ref, *, rows):
    j = pl.program_id(2)
    start = pl.multiple_of(_na_band_start(j, rows) * GRID_W, GRID_W)
    band = NA_BAND * GRID_W
    k = jnp.concatenate([k_ref[0, pl.ds(start, band), :], kc_ref[0]], axis=0)
    v = jnp.concatenate([v_ref[0, pl.ds(start, band), :], vc_ref[0]], axis=0)
    o = _group_attention(q_ref[0], k, v, lambda s: bias_ref[0, s])
    o_ref[0] = o.astype(o_ref.dtype)


def _na_attention(q, k, v, k_ctx, v_ctx, bias):
    b, l, w = q.shape
    rows = l // GRID_W
    tq = NA_QROWS * GRID_W
    nq = l // tq
    lc = k_ctx.shape[1]
    tk = bias.shape[-1]

    def bias_index(bi, gi, ji):
        variant = jnp.where(ji == 0, 0, jnp.where(ji == nq - 1, 2, 1))
        return (variant, gi, 0, 0)

    return pl.pallas_call(
        functools.partial(_na_kernel, rows=rows),
        grid=(b, w // MXU_DIM, nq),
        in_specs=[
            pl.BlockSpec((1, tq, MXU_DIM), lambda bi, gi, ji: (bi, ji, gi)),
            pl.BlockSpec((1, l, MXU_DIM), lambda bi, gi, ji: (bi, 0, gi)),
            pl.BlockSpec((1, l, MXU_DIM), lambda bi, gi, ji: (bi, 0, gi)),
            pl.BlockSpec((1, lc, MXU_DIM), lambda bi, gi, ji: (bi, 0, gi)),
            pl.BlockSpec((1, lc, MXU_DIM), lambda bi, gi, ji: (bi, 0, gi)),
            pl.BlockSpec((1, HEADS_PER_GROUP, tq, tk), bias_index),
        ],
        out_specs=pl.BlockSpec((1, tq, MXU_DIM), lambda bi, gi, ji: (bi, ji, gi)),
        out_shape=jax.ShapeDtypeStruct((b, l, w), BF16),
        compiler_params=_cparams(3),
        name="na_attention",
    )(q, k, v, k_ctx, v_ctx, bias)


def _na_bias(rpb, rows, lc):
    nq = rows // NA_QROWS
    qr = jnp.arange(NA_QROWS)[:, None, None, None]
    w = jnp.arange(GRID_W)[None, :, None, None]
    kj = jnp.arange(NA_BAND)[None, None, :, None]
    kc = jnp.arange(GRID_W)[None, None, None, :]
    cs = jnp.clip(w - WIN_W // 2, 0, GRID_W - WIN_W)
    col_ok = (kc >= cs) & (kc < cs + WIN_W)
    dc = jnp.clip(kc - w + WIN_W - 1, 0, 2 * WIN_W - 2)
    variants = []
    for jb in (0, 1, nq - 1):
        r = jb * NA_QROWS + qr
        kr = _na_band_start(jb, rows) + kj
        rs = jnp.clip(r - WIN_H // 2, 0, rows - WIN_H)
        ok = (kr >= rs) & (kr < rs + WIN_H) & col_ok
        dr = jnp.clip(kr - r + WIN_H - 1, 0, 2 * WIN_H - 2)
        bias = jnp.where(ok[None], rpb[:, dr, dc].astype(F32), MASK_VALUE)
        bias = bias.reshape(NA_HEADS, NA_QROWS * GRID_W, NA_BAND * GRID_W)
        variants.append(jnp.concatenate([bias, jnp.zeros(bias.shape[:2] + (lc,), F32)], axis=-1))
    return jnp.stack(variants)


def _ssm_prep_kernel(are_ref, aim_ref, ldt_ref, bre_ref, bim_ref, cre_ref, cim_ref,
                     w1_ref, qt_ref, pw_ref, ba_re, ba_im, rt, ct):
    for d in range(2):
        a_re = are_ref[d, 0]
        a_im = aim_ref[d, 0]
        dt = jnp.exp(ldt_ref[d, 0])
        lam_re = a_re * dt
        lam_im = a_im * dt
        mag = jnp.exp(lam_re)
        abr = mag * jnp.cos(lam_im)
        abi = mag * jnp.sin(lam_im)
        den = a_re * a_re + a_im * a_im
        nr = abr - 1.0
        ni = abi
        kr = (nr * a_re + ni * a_im) / den
        ki = (ni * a_re - nr * a_im) / den
        b_re = bre_ref[d, 0]
        b_im = bim_ref[d, 0]
        bb_re = kr * b_re - ki * b_im
        bb_im = kr * b_im + ki * b_re
        c_re = cre_ref[d, 0]
        c_im = cim_ref[d, 0]

        def power(n, lam_re=lam_re, lam_im=lam_im):
            m = jnp.exp(lam_re * n)
            return m * jnp.cos(lam_im * n), m * jnp.sin(lam_im * n)

        for s in range(CHUNK):
            p_re, p_im = power(float(CHUNK - 1 - s if d == 0 else s))
            rows = slice(SSM_GROUP * s, SSM_GROUP * (s + 1))
            ba_re[rows, :] = bb_re * p_re - bb_im * p_im
            ba_im[rows, :] = bb_re * p_im + bb_im * p_re
            o_re, o_im = power(float(s + 1 if d == 0 else CHUNK - s))
            qt_ref[d, 0, rows, :SSM_STATE] = (c_re * o_re - c_im * o_im).astype(qt_ref.dtype)
            qt_ref[d, 0, rows, SSM_STATE:] = (-(c_re * o_im + c_im * o_re)).astype(qt_ref.dtype)
        rt[:, :SSM_STATE] = ba_re[...]
        rt[:, SSM_STATE:] = -ba_im[...]
        ct[:, :SSM_STATE] = c_re
        ct[:, SSM_STATE:] = c_im
        kt = lax.dot_general(rt[...], ct[...], (((1,), (1,)), ((), ())), preferred_element_type=F32,
                             precision=lax.Precision.HIGHEST)
        base = d * (MXU_DIM + 2 * SSM_STATE)
        w1_ref[0, :, base:base + MXU_DIM] = jnp.zeros((MXU_DIM, MXU_DIM), w1_ref.dtype)
        for t in range(CHUNK):
            cols = slice(base + SSM_GROUP * t, base + SSM_GROUP * (t + 1))
            n = SSM_GROUP * (t + 1) if d == 0 else SSM_GROUP * (CHUNK - t)
            if d == 0:
                w1_ref[0, :n, cols] = kt[MXU_DIM - n:, :].astype(w1_ref.dtype)
            else:
                w1_ref[0, MXU_DIM - n:, cols] = kt[:n, :].astype(w1_ref.dtype)
        w1_ref[0, :, base + MXU_DIM:base + MXU_DIM + SSM_STATE] = ba_re[...].astype(w1_ref.dtype)
        w1_ref[0, :, base + MXU_DIM + SSM_STATE:base + MXU_DIM + 2 * SSM_STATE] = ba_im[...].astype(w1_ref.dtype)
        for j in range(8):
            s_re, s_im = power(float(CHUNK * 2 ** j))
            pw_ref[d, 0, 0, j:j + 1, :SSM_STATE] = s_re
            pw_ref[d, 0, 0, j:j + 1, SSM_STATE:] = s_re
            pw_ref[d, 0, 1, j:j + 1, :SSM_STATE] = -s_im
            pw_ref[d, 0, 1, j:j + 1, SSM_STATE:] = s_im


def _ssm_prep(a_re, a_im, log_dt, b_re, b_im, c_re, c_im):
    g, p, c = SSM_GROUPS, SSM_STATE, SSM_GROUP
    wcols = 2 * (MXU_DIM + 2 * p)
    vec = lambda a: a.reshape(2, g, 1, p)
    mat = pl.BlockSpec((2, 1, c, p), lambda i: (0, i, 0, 0))
    return pl.pallas_call(
        _ssm_prep_kernel,
        grid=(g,),
        in_specs=[
            pl.BlockSpec((2, 1, 1, p), lambda i: (0, i, 0, 0)),
            pl.BlockSpec((2, 1, 1, p), lambda i: (0, i, 0, 0)),
            pl.BlockSpec((2, 1, 1, 1), lambda i: (0, i, 0, 0)),
            mat, mat, mat, mat,
        ],
        out_specs=[
            pl.BlockSpec((1, MXU_DIM, wcols), lambda i: (i, 0, 0)),
            pl.BlockSpec((2, 1, MXU_DIM, 2 * p), lambda i: (0, i, 0, 0)),
            pl.BlockSpec((2, 1, 2, SUBLANES, 2 * p), lambda i: (0, i, 0, 0, 0)),
        ],
        out_shape=[
            jax.ShapeDtypeStruct((g, MXU_DIM, wcols), F32),
            jax.ShapeDtypeStruct((2, g, MXU_DIM, 2 * p), F32),
            jax.ShapeDtypeStruct((2, g, 2, SUBLANES, 2 * p), F32),
        ],
        scratch_shapes=[
            pltpu.VMEM((MXU_DIM, p), F32),
            pltpu.VMEM((MXU_DIM, p), F32),
            pltpu.VMEM((MXU_DIM, 2 * p), F32),
            pltpu.VMEM((c, 2 * p), F32),
        ],
        compiler_params=_cparams(1),
        name="ssm_prep",
    )(vec(a_re), vec(a_im), log_dt.reshape(2, g, 1, 1),
      jnp.swapaxes(b_re, -1, -2), jnp.swapaxes(b_im, -1, -2), c_re, c_im)


def _cmul(x, c_rr, c_mi):
    return x * c_rr + pltpu.roll(x, SSM_STATE, 1) * c_mi


def _ssm_kernel(u_ref, w1_ref, qt_ref, pw_ref, d_ref, h0_ref, y_ref, fin_ref, x_scr, *, seg):
    nseg = CHUNK_ROWS // seg
    p2 = 2 * SSM_STATE
    u = u_ref[0, 0]
    r = jnp.dot(u.astype(BF16), w1_ref[0], preferred_element_type=F32)
    row = lax.broadcasted_iota(jnp.int32, (CHUNK_ROWS, 1), 0)
    pos = row & (seg - 1)
    y = u * d_ref[0]
    for d in range(2):
        base = d * (MXU_DIM + p2)
        y = y + r[:, base:base + MXU_DIM]
        x = r[:, base + MXU_DIM:base + MXU_DIM + p2]
        h0 = h0_ref[0, 0, :, d * p2:(d + 1) * p2]
        if nseg == 1:
            h0_rows = jnp.broadcast_to(h0, (CHUNK_ROWS, p2))
        else:
            h0_rows = jnp.broadcast_to(h0[:, None, :], (nseg, seg, p2)).reshape(CHUNK_ROWS, p2)
        entry = (pos == 0) if d == 0 else (pos == seg - 1)
        x = x + jnp.where(entry, _cmul(h0_rows, pw_ref[d, 0, 0, 0:1, :], pw_ref[d, 0, 1, 0:1, :]), 0.0)
        for j in range(int(math.log2(seg))):
            n = 2 ** j
            if d == 0:
                sh = jnp.where(pos >= n, pltpu.roll(x, n, 0), 0.0)
            else:
                sh = jnp.where(pos < seg - n, pltpu.roll(x, CHUNK_ROWS - n, 0), 0.0)
            x = x + _cmul(sh, pw_ref[d, 0, 0, j:j + 1, :], pw_ref[d, 0, 1, j:j + 1, :])
        x_scr[d] = x
        if d == 0:
            carried = jnp.where(entry, h0_rows, pltpu.roll(x, 1, 0))
            fin = x_scr[d, pl.ds(seg - 1, nseg, stride=seg), :]
        else:
            carried = jnp.where(entry, h0_rows, pltpu.roll(x, CHUNK_ROWS - 1, 0))
            fin = x_scr[d, pl.ds(0, nseg, stride=seg), :]
        fin_ref[0, 0, :, d * p2:(d + 1) * p2] = fin
        y = y + lax.dot_general(carried.astype(BF16), qt_ref[d, 0], (((1,), (1,)), ((), ())),
                                preferred_element_type=F32)
    y_ref[0, 0] = y


def _ssm(uc, w1, qt, pw, dtile, h0, *, seg):
    b, g = uc.shape[:2]
    nseg = CHUNK_ROWS // seg
    p2 = 2 * SSM_STATE
    return pl.pallas_call(
        functools.partial(_ssm_kernel, seg=seg),
        grid=(b, g),
        in_specs=[
            pl.BlockSpec((1, 1, CHUNK_ROWS, MXU_DIM), lambda bi, gi: (bi, gi, 0, 0)),
            pl.BlockSpec((1,) + w1.shape[1:], lambda bi, gi: (gi, 0, 0)),
            pl.BlockSpec((2, 1, MXU_DIM, p2), lambda bi, gi: (0, gi, 0, 0)),
            pl.BlockSpec((2, 1, 2, SUBLANES, p2), lambda bi, gi: (0, gi, 0, 0, 0)),
            pl.BlockSpec((1, 1, MXU_DIM), lambda bi, gi: (gi, 0, 0)),
            pl.BlockSpec((1, 1, nseg, 2 * p2), lambda bi, gi: (bi, gi, 0, 0)),
        ],
        out_specs=[
            pl.BlockSpec((1, 1, CHUNK_ROWS, MXU_DIM), lambda bi, gi: (bi, gi, 0, 0)),
            pl.BlockSpec((1, 1, nseg, 2 * p2), lambda bi, gi: (bi, gi, 0, 0)),
        ],
        out_shape=[
            jax.ShapeDtypeStruct(uc.shape, F32),
            jax.ShapeDtypeStruct((b, g, nseg, 2 * p2), F32),
        ],
        scratch_shapes=[pltpu.VMEM((2, CHUNK_ROWS, p2), F32)],
        compiler_params=_cparams(2),
        name="ssm_scan",
    )(uc, w1, qt, pw, dtile, h0)


def _to_chunks(u, nb):
    u = u.reshape(nb, CHUNK_ROWS, CHUNK, SSM_GROUPS, SSM_GROUP)
    return u.transpose(0, 3, 1, 2, 4).reshape(nb, SSM_GROUPS, CHUNK_ROWS, MXU_DIM)


def _from_chunks(y):
    nb = y.shape[0]
    y = y.reshape(nb, SSM_GROUPS, CHUNK_ROWS, CHUNK, SSM_GROUP)
    return y.transpose(0, 2, 3, 1, 4).reshape(nb * CHUNK_ROWS * CHUNK, SSM_WIDTH)


def _rope_tables(length):
    t = jnp.arange(length)
    nf = HEAD_DIM // 4
    freqs = ROPE_THETA ** (-jnp.arange(nf, dtype=F32) / nf)
    lane = jnp.arange(MXU_DIM)
    pos = jnp.where((lane % HEAD_DIM) < HEAD_DIM // 2, (t // GRID_W)[:, None], (t % GRID_W)[:, None]).astype(F32)
    ang = pos * freqs[lane % nf][None, :]
    sign = jnp.where((lane % (2 * nf)) < nf, -1.0, 1.0)[None, :]
    return jnp.cos(ang), jnp.sin(ang) * sign


def kernel(x_prompt, x_sample, cache_na_k, cache_na_v, state_ssm_re, state_ssm_im, cache_gqa_k, cache_gqa_v,
           c, c_ctx, norm1_g, norm2_g, ada_w, ada_b, ffn_w_up, ffn_conv_w, ffn_conv_b, ffn_w_down,
           ev_w_in, ev_w_out, na_q_g, na_k_g, na_rpb, ssm_a_re, ssm_a_im, ssm_log_dt, ssm_b_re, ssm_b_im,
           ssm_c_re, ssm_c_im, ssm_d, ssm_w_glu, ssm_b_glu, od_w_in, od_w_out, gqa_q_g, gqa_k_g):
    nb_c, s_c, _ = x_prompt.shape
    nb_l, s_l, _ = x_sample.shape
    past = cache_na_k.shape[2]
    rows = s_l // GRID_W
    assert s_c == CHUNK * CHUNK and nb_c * (s_c // CHUNK) == CHUNK_ROWS and s_l // CHUNK == CHUNK_ROWS
    assert rows % NA_QROWS == 0 and rows >= NA_BAND and nb_l + 1 <= SUBLANES

    tm_c, tpb_c = s_c, 1
    tm_l = 512
    tpb_l = s_l // tm_l

    cond = jnp.concatenate([c_ctx[None], c, jnp.zeros((SUBLANES - 1 - nb_l, D_MODEL), F32)], axis=0)
    mods = _ada(cond, ada_w, ada_b)
    ones_bd = jnp.kron(jnp.eye(HEADS_PER_GROUP, dtype=F32), jnp.ones((HEAD_DIM, HEAD_DIM), F32)).astype(BF16)
    rope_tables = _rope_tables(s_l)

    perm = jnp.array([GQA_KV_HEADS * s + i for i in range(GQA_HEADS // GQA_KV_HEADS) for s in range(GQA_KV_HEADS)])
    qw = GQA_HEADS * HEAD_DIM

    xc = x_prompt.reshape(nb_c * s_c, D_MODEL)
    xl = x_sample.reshape(nb_l * s_l, D_MODEL)
    new_na_k, new_na_v, new_s_re, new_s_im, new_g_k, new_g_v = [], [], [], [], [], []

    for l in range(DEPTH):
        i = l // 2
        mod_c = jnp.broadcast_to(mods[l, 0].reshape(1, 6, D_MODEL), (nb_c, 6, D_MODEL))
        mod_l = mods[l, 1:1 + nb_l].reshape(nb_l, 6, D_MODEL)
        g1 = norm1_g[l].reshape(1, D_MODEL)
        g2 = norm2_g[l].reshape(1, D_MODEL)
        if l % 2 == 0:
            w_in = ev_w_in[i].astype(BF16)
            w_out = ev_w_out[i].astype(BF16)
            w_glu = ssm_w_glu[i].astype(BF16)
            b_glu = ssm_b_glu[i].reshape(1, SSM_WIDTH)
            qg = jnp.tile(na_q_g[i], NA_HEADS).reshape(1, NA_WIDTH)
            kg = jnp.tile(na_k_g[i], NA_HEADS).reshape(1, NA_WIDTH)
            w1, qt, pw = _ssm_prep(ssm_a_re[i], ssm_a_im[i], ssm_log_dt[i], ssm_b_re[i], ssm_b_im[i],
                                   ssm_c_re[i], ssm_c_im[i])
            w1 = w1.astype(BF16)
            qt = qt.astype(BF16)
            dtile = jnp.tile(ssm_d[i], (1, CHUNK)).reshape(SSM_GROUPS, 1, MXU_DIM)

            q, k, v, u = _even_in(xc, g1, mod_c, w_in, ones_bd, qg, kg, tm=tm_c, tiles_per_batch=tpb_c, kv_dtype=F32)
            new_na_k.append(k.reshape(nb_c, s_c, NA_HEADS, HEAD_DIM))
            new_na_v.append(v.reshape(nb_c, s_c, NA_HEADS, HEAD_DIM))
            na = _attention(q.reshape(nb_c, s_c, NA_WIDTH), k.reshape(nb_c, s_c, NA_WIDTH),
                            v.reshape(nb_c, s_c, NA_WIDTH), tq=s_c, shared_kv=False)
            h0 = jnp.zeros((1, SSM_GROUPS, nb_c, 4 * SSM_STATE), F32)
            ys, fin = _ssm(_to_chunks(u, 1), w1, qt, pw, dtile, h0, seg=s_c // CHUNK)
            fin = fin.reshape(SSM_GROUPS, nb_c, 2, 2, SSM_STATE).transpose(3, 1, 2, 0, 4)
            new_s_re.append(fin[0])
            new_s_im.append(fin[1])
            xc = _even_out(xc, mod_c, na.reshape(nb_c * s_c, NA_WIDTH), _from_chunks(ys), w_glu, b_glu, w_out,
                           tm=tm_c, tiles_per_batch=tpb_c)

            q, k, v, u = _even_in(xl, g1, mod_l, w_in, ones_bd, qg, kg, tm=tm_l, tiles_per_batch=tpb_l,
                                  kv_dtype=BF16)
            bias = _na_bias(na_rpb[i], rows, past)
            na = _na_attention(q.reshape(nb_l, s_l, NA_WIDTH), k.reshape(nb_l, s_l, NA_WIDTH),
                               v.reshape(nb_l, s_l, NA_WIDTH),
                               cache_na_k[:, i].reshape(nb_l, past, NA_WIDTH).astype(BF16),
                               cache_na_v[:, i].reshape(nb_l, past, NA_WIDTH).astype(BF16), bias)
            h0 = jnp.concatenate([state_ssm_re[:, i, 0], state_ssm_im[:, i, 0],
                                  state_ssm_re[:, i, 1], state_ssm_im[:, i, 1]], axis=-1)
            ys, _ = _ssm(_to_chunks(u, nb_l), w1, qt, pw, dtile, h0[:, :, None, :], seg=s_l // CHUNK)
            xl = _even_out(xl, mod_l, na.reshape(nb_l * s_l, NA_WIDTH), _from_chunks(ys), w_glu, b_glu, w_out,
                           tm=tm_l, tiles_per_batch=tpb_l)
        else:
            w_q = od_w_in[i][:, :qw].reshape(D_MODEL, GQA_HEADS, HEAD_DIM)[:, perm].reshape(D_MODEL, qw)
            w_in = jnp.concatenate([w_q, od_w_in[i][:, qw:]], axis=1).astype(BF16)
            w_out = od_w_out[i].reshape(GQA_HEADS, HEAD_DIM, D_MODEL)[perm].reshape(qw, D_MODEL).astype(BF16)
            qg = jnp.tile(gqa_q_g[i], GQA_HEADS).reshape(1, qw)
            kg = jnp.tile(gqa_k_g[i], GQA_KV_HEADS).reshape(1, MXU_DIM)

            q, k, v = _odd_in(xc, g1, mod_c, w_in, ones_bd, qg, kg, None, tm=tm_c, tiles_per_batch=tpb_c,
                              kv_dtype=F32)
            new_g_k.append(k.reshape(nb_c, s_c, GQA_KV_HEADS, HEAD_DIM))
            new_g_v.append(v.reshape(nb_c, s_c, GQA_KV_HEADS, HEAD_DIM))
            a = _attention(q.reshape(nb_c, s_c, qw), k.reshape(nb_c, s_c, MXU_DIM), v.reshape(nb_c, s_c, MXU_DIM),
                           tq=s_c, shared_kv=True)
            xc = _odd_out(xc, mod_c, a.reshape(nb_c * s_c, qw), w_out, tm=tm_c, tiles_per_batch=tpb_c)

            q, k, v = _odd_in(xl, g1, mod_l, w_in, ones_bd, qg, kg, rope_tables, tm=tm_l, tiles_per_batch=tpb_l,
                              kv_dtype=BF16)
            k_all = jnp.concatenate([k.reshape(nb_l, s_l, MXU_DIM),
                                     cache_gqa_k[:, i].reshape(nb_l, past, MXU_DIM).astype(BF16)], axis=1)
            v_all = jnp.concatenate([v.reshape(nb_l, s_l, MXU_DIM),
                                     cache_gqa_v[:, i].reshape(nb_l, past, MXU_DIM).astype(BF16)], axis=1)
            a = _attention(q.reshape(nb_l, s_l, qw), k_all, v_all, tq=256, shared_kv=True)
            xl = _odd_out(xl, mod_l, a.reshape(nb_l * s_l, qw), w_out, tm=tm_l, tiles_per_batch=tpb_l)

        w_up = ffn_w_up[l].astype(BF16)
        w_dn = ffn_w_down[l].astype(BF16)
        cb = ffn_conv_b[l].reshape(1, D_FF)
        xc = _ffn(xc, g2, mod_c, w_up, ffn_conv_w[l], cb, w_dn, tm=tm_c, tiles_per_batch=tpb_c)
        xl = _ffn(xl, g2, mod_l, w_up, ffn_conv_w[l], cb, w_dn, tm=tm_l, tiles_per_batch=tpb_l)

    return (xc.reshape(nb_c, s_c, D_MODEL), xl.reshape(nb_l, s_l, D_MODEL),
            jnp.stack(new_na_k, axis=1), jnp.stack(new_na_v, axis=1),
            jnp.stack(new_s_re, axis=1), jnp.stack(new_s_im, axis=1),
            jnp.stack(new_g_k, axis=1), jnp.stack(new_g_v, axis=1))
```
